```python
import math
import jax, jax.numpy as jnp
from jax import lax
import numpy as np

D_MODEL = 1024
BATCH = 8
SEQ = 4096
DEPTH = 4

GRID_W = 64
CTX_LEN = 256
N_MIXERS = 3
N_MOD = 9
ROPE_THETA = 10000.0
NORM_EPS = 1e-6
Q_BLOCK = 128
NEG_INF = -1e30
D_FF = ((8 * D_MODEL // 3 + 127) // 128) * 128

A_HEADS = D_MODEL // 128
A_NOPE = 128
A_ROPE = 64
A_V = 128
A_Q_LORA = 3 * D_MODEL // 8
A_KV_LORA = D_MODEL // 4
A_IN = A_Q_LORA + A_KV_LORA + A_ROPE

B_HEAD_DIM = 64
B_HEADS = D_MODEL // (2 * B_HEAD_DIM)
B_WIDTH = B_HEADS * 2 * B_HEAD_DIM

C_HEAD_DIM = 64
C_HEADS = D_MODEL // C_HEAD_DIM
C_KV_HEADS = C_HEADS // 4
C_GROUP = C_HEADS // C_KV_HEADS
C_WINDOW = 128

N_A = len(range(0, DEPTH, N_MIXERS))
N_B = len(range(1, DEPTH, N_MIXERS))
N_C = len(range(2, DEPTH, N_MIXERS))

kernel_name = "hybrid_dit_mla_diff_swa_macaron"


def rmsnorm(x, g):
    xf = x.astype(jnp.float32)
    xf = xf * lax.rsqrt(jnp.mean(jnp.square(xf), axis=-1, keepdims=True) + NORM_EPS)
    return (xf * g.astype(jnp.float32)).astype(x.dtype)


def modulate(h, shift, scale):
    return h * (1.0 + scale) + shift


def swiglu(h, w_in, w_out):
    gate, up = jnp.split(h @ w_in, 2, axis=-1)
    return (jax.nn.silu(gate) * up) @ w_out


def ffn_half_step(x, g, shift, scale, gate, w_in, w_out):
    return x + 0.5 * gate * swiglu(modulate(rmsnorm(x, g), shift, scale), w_in, w_out)


def merge_heads(o):
    Bn, H, L, d = o.shape
    return o.transpose(0, 2, 1, 3).reshape(Bn, L, H * d)


def softmax_f32(s, scale):
    return jax.nn.softmax(s.astype(jnp.float32) * scale, axis=-1)


def axial_rope_tables(row, col, dim, dtype):
    quarter = dim // 4
    inv_freq = ROPE_THETA ** (-jnp.arange(quarter, dtype=jnp.float32) / quarter)
    ang = jnp.stack([row.astype(jnp.float32)[:, None] * inv_freq,
                     col.astype(jnp.float32)[:, None] * inv_freq], axis=1)
    return jnp.cos(ang).astype(dtype), jnp.sin(ang).astype(dtype)


def apply_axial_rope(x, cos, sin):
    q = x.shape[-1] // 4
    xs = x.reshape(*x.shape[:-1], 2, 2, q)
    x1, x2 = xs[..., 0, :], xs[..., 1, :]
    out = jnp.stack([x1 * cos - x2 * sin, x2 * cos + x1 * sin], axis=-2)
    return out.reshape(x.shape)


def sweep_query_blocks(fn, q):
    *lead, S, d = q.shape
    nb = S // Q_BLOCK
    blocks = jnp.moveaxis(q.reshape(*lead, nb, Q_BLOCK, d), -3, 0)
    out = jnp.moveaxis(lax.map(fn, blocks), 0, -3)
    return out.reshape(*out.shape[:-3], nb * Q_BLOCK, out.shape[-1])


def mla_mixer(h_lat, h_ctx, row, col, w_in, g_q, g_kv, w_qb, w_kvb, w_o, need_ctx_out):
    scale = (A_NOPE + A_ROPE) ** -0.5
    cos, sin = axial_rope_tables(row, col, A_ROPE, h_lat.dtype)

    def compress(h):
        a = h @ w_in
        return (a[..., :A_Q_LORA],
                rmsnorm(a[..., A_Q_LORA:A_Q_LORA + A_KV_LORA], g_kv),
                a[..., A_Q_LORA + A_KV_LORA:])

    def queries(cq):
        Bn, L, _ = cq.shape
        return (rmsnorm(cq, g_q) @ w_qb).reshape(Bn, L, A_HEADS, A_NOPE + A_ROPE).transpose(0, 2, 1, 3)

    def keys_values(ckv, k_rope):
        Bn, L, _ = ckv.shape
        kv = (ckv @ w_kvb).reshape(Bn, L, A_HEADS, A_NOPE + A_V).transpose(0, 2, 1, 3)
        k_rope = jnp.broadcast_to(k_rope[:, None], (Bn, A_HEADS, L, A_ROPE))
        return jnp.concatenate([kv[..., :A_NOPE], k_rope], axis=-1), kv[..., A_NOPE:]

    cq_l, ckv_l, kr_l = compress(h_lat)
    cq_c, ckv_c, kr_c = compress(h_ctx)
    q_l = queries(cq_l)
    q_l = jnp.concatenate([q_l[..., :A_NOPE], apply_axial_rope(q_l[..., A_NOPE:], cos, sin)], axis=-1)
    k_l, v_l = keys_values(ckv_l, apply_axial_rope(kr_l, cos, sin))
    k_c, v_c = keys_values(ckv_c, kr_c)
    k_all = jnp.concatenate([k_c, k_l], axis=2)
    v_all = jnp.concatenate([v_c, v_l], axis=2)

    def attend(qb):
        p = softmax_f32(jnp.einsum('bhqd,bhkd->bhqk', qb, k_all), scale)
        return jnp.einsum('bhqk,bhkd->bhqd', p.astype(v_all.dtype), v_all)

    y_lat = merge_heads(sweep_query_blocks(attend, q_l)) @ w_o
    y_ctx = None
    if need_ctx_out:
        q_c = queries(cq_c)
        p = softmax_f32(jnp.einsum('bhqd,bhkd->bhqk', q_c, k_c), scale)
        y_ctx = merge_heads(jnp.einsum('bhqk,bhkd->bhqd', p.astype(v_c.dtype), v_c)) @ w_o
    return y_lat, y_ctx


def diff_mixer(h_lat, h_ctx, row, col, w_qkv, lam_params, g_sub, w_o, layer_idx, need_ctx_out):
    d = B_HEAD_DIM
    scale = d ** -0.5
    lam_init = 0.8 - 0.6 * math.exp(-0.3 * layer_idx)
    lp = lam_params.astype(jnp.float32)
    lam = jnp.exp(jnp.sum(lp[0] * lp[1])) - jnp.exp(jnp.sum(lp[2] * lp[3])) + lam_init
    cos, sin = axial_rope_tables(row, col, d, h_lat.dtype)

    def project(h):
        Bn, L, _ = h.shape
        qkv = h @ w_qkv
        q = qkv[..., :B_WIDTH].reshape(Bn, L, B_HEADS, 2, d).transpose(0, 2, 3, 1, 4)
        k = qkv[..., B_WIDTH:2 * B_WIDTH].reshape(Bn, L, B_HEADS, 2, d).transpose(0, 2, 3, 1, 4)
        v = qkv[..., 2 * B_WIDTH:].reshape(Bn, L, B_HEADS, 2 * d).transpose(0, 2, 1, 3)
        return q, k, v

    def diff_attend(q, k, v):
        p = softmax_f32(jnp.einsum('bhmqd,bhmkd->bhmqk', q, k), scale)
        a = p[:, :, 0] - lam * p[:, :, 1]
        return jnp.einsum('bhqk,bhkd->bhqd', a.astype(v.dtype), v)

    def finish(o):
        return merge_heads(rmsnorm(o, g_sub) * (1.0 - lam_init)) @ w_o

    q_l, k_l, v_l = project(h_lat)
    q_c, k_c, v_c = project(h_ctx)
    q_l = apply_axial_rope(q_l, cos, sin)
    k_l = apply_axial_rope(k_l, cos, sin)
    k_all = jnp.concatenate([k_c, k_l], axis=3)
    v_all = jnp.concatenate([v_c, v_l], axis=2)
    y_lat = finish(sweep_query_blocks(lambda qb: diff_attend(qb, k_all, v_all), q_l))
    y_ctx = finish(diff_attend(q_c, k_c, v_c)) if need_ctx_out else None
    return y_lat, y_ctx


def swa_mixer(h_lat, h_ctx, row, col, w_qkv, sink, w_o, need_ctx_out):
    d = C_HEAD_DIM
    scale = d ** -0.5
    S = h_lat.shape[1]
    cos, sin = axial_rope_tables(row, col, d, h_lat.dtype)
    nq, nkv = C_HEADS * d, C_KV_HEADS * d

    def project(h):
        Bn, L, _ = h.shape
        qkv = h @ w_qkv
        q = qkv[..., :nq].reshape(Bn, L, C_KV_HEADS, C_GROUP, d).transpose(0, 2, 3, 1, 4)
        k = qkv[..., nq:nq + nkv].reshape(Bn, L, C_KV_HEADS, d).transpose(0, 2, 1, 3)
        v = qkv[..., nq + nkv:].reshape(Bn, L, C_KV_HEADS, d).transpose(0, 2, 1, 3)
        return q, k, v

    sink_logit = sink.astype(jnp.float32).reshape(C_KV_HEADS, C_GROUP)[None, :, :, None, None]

    def sink_softmax(logits):
        sl = jnp.broadcast_to(sink_logit, logits.shape[:-1] + (1,))
        return jax.nn.softmax(jnp.concatenate([sl, logits], axis=-1), axis=-1)[..., 1:]

    def merge(o):
        Bn, _, _, L, _ = o.shape
        return o.transpose(0, 3, 1, 2, 4).reshape(Bn, L, nq) @ w_o

    q_l, k_l, v_l = project(h_lat)
    q_c, k_c, v_c = project(h_ctx)
    q_l = apply_axial_rope(q_l, cos, sin)
    k_l = apply_axial_rope(k_l, cos, sin)
    n_ctx = k_c.shape[2]
    pad = ((0, 0), (0, 0), (Q_BLOCK, Q_BLOCK), (0, 0))
    k_pad = jnp.pad(k_l, pad)
    v_pad = jnp.pad(v_l, pad)
    offs_q = jnp.arange(Q_BLOCK, dtype=jnp.int32)
    offs_k = jnp.arange(3 * Q_BLOCK, dtype=jnp.int32) - Q_BLOCK

    def band_block(start):
        qb = lax.dynamic_slice_in_dim(q_l, start, Q_BLOCK, axis=3)
        kb = lax.dynamic_slice_in_dim(k_pad, start, 3 * Q_BLOCK, axis=2)
        vb = lax.dynamic_slice_in_dim(v_pad, start, 3 * Q_BLOCK, axis=2)
        key_pos = start + offs_k
        valid = ((jnp.abs(offs_k[None, :] - offs_q[:, None]) <= C_WINDOW)
                 & (key_pos >= 0)[None, :] & (key_pos < S)[None, :])
        s_ctx = jnp.einsum('bhgqd,bhkd->bhgqk', qb, k_c).astype(jnp.float32) * scale
        s_win = jnp.where(valid, jnp.einsum('bhgqd,bhkd->bhgqk', qb, kb).astype(jnp.float32) * scale, NEG_INF)
        p = sink_softmax(jnp.concatenate([s_ctx, s_win], axis=-1)).astype(v_c.dtype)
        return (jnp.einsum('bhgqk,bhkd->bhgqd', p[..., :n_ctx], v_c)
                + jnp.einsum('bhgqk,bhkd->bhgqd', p[..., n_ctx:], vb))

    o = lax.map(band_block, jnp.arange(S // Q_BLOCK, dtype=jnp.int32) * Q_BLOCK)
    o = jnp.moveaxis(o, 0, 3)
    o = o.reshape(*o.shape[:3], S, d)
    y_lat = merge(o)
    y_ctx = None
    if need_ctx_out:
        p = sink_softmax(jnp.einsum('bhgqd,bhkd->bhgqk', q_c, k_c).astype(jnp.float32) * scale)
        y_ctx = merge(jnp.einsum('bhgqk,bhkd->bhgqd', p.astype(v_c.dtype), v_c))
    return y_lat, y_ctx


def setup_inputs(seed: int = 0) -> dict:
    key = jax.random.key(seed)
    ks = jax.random.split(key, 24)

    def nrm(k, shape, s):
        return jax.random.normal(k, shape, jnp.float32) * s

    def gain(k, shape):
        return 1.0 + 0.05 * jax.random.normal(k, shape, jnp.float32)

    return {
        "x": nrm(ks[0], (BATCH, SEQ, D_MODEL), 1.0),
        "c": nrm(ks[1], (BATCH, D_MODEL), 1.0),
        "ctx": nrm(ks[2], (BATCH, CTX_LEN, D_MODEL), 1.0),
        "c_ctx": nrm(ks[3], (D_MODEL,), 1.0),
        "w_mod": nrm(ks[4], (DEPTH, D_MODEL, N_MOD * D_MODEL), 0.5 * D_MODEL ** -0.5),
        "b_mod": nrm(ks[5], (DEPTH, N_MOD * D_MODEL), 0.02),
        "g_norm": gain(ks[6], (DEPTH, 3, D_MODEL)),
        "w_ffn_in": nrm(ks[7], (DEPTH, 2, D_MODEL, 2 * D_FF), D_MODEL ** -0.5),
        "w_ffn_out": nrm(ks[8], (DEPTH, 2, D_FF, D_MODEL), D_FF ** -0.5),
        "a_w_in": nrm(ks[9], (N_A, D_MODEL, A_IN), D_MODEL ** -0.5),
        "a_g_q": gain(ks[10], (N_A, A_Q_LORA)),
        "a_g_kv": gain(ks[11], (N_A, A_KV_LORA)),
        "a_w_qb": nrm(ks[12], (N_A, A_Q_LORA, A_HEADS * (A_NOPE + A_ROPE)), A_Q_LORA ** -0.5),
        "a_w_kvb": nrm(ks[13], (N_A, A_KV_LORA, A_HEADS * (A_NOPE + A_V)), A_KV_LORA ** -0.5),
        "a_w_o": nrm(ks[14], (N_A, A_HEADS * A_V, D_MODEL), (A_HEADS * A_V) ** -0.5),
        "b_w_qkv": nrm(ks[15], (N_B, D_MODEL, 3 * B_WIDTH), D_MODEL ** -0.5),
        "b_lambda": nrm(ks[16], (N_B, 4, B_HEAD_DIM), 0.1),
        "b_g_sub": gain(ks[17], (N_B, 2 * B_HEAD_DIM)),
        "b_w_o": nrm(ks[18], (N_B, B_WIDTH, D_MODEL), B_WIDTH ** -0.5),
        "c_w_qkv": nrm(ks[19], (N_C, D_MODEL, (C_HEADS + 2 * C_KV_HEADS) * C_HEAD_DIM), D_MODEL ** -0.5),
        "c_sink": nrm(ks[20], (N_C, C_HEADS), 1.0),
        "c_w_o": nrm(ks[21], (N_C, C_HEADS * C_HEAD_DIM, D_MODEL), (C_HEADS * C_HEAD_DIM) ** -0.5),
        "g_final": gain(ks[22], (D_MODEL,)),
    }


def reference(x, c, ctx, c_ctx, w_mod, b_mod, g_norm, w_ffn_in, w_ffn_out,
              a_w_in, a_g_q, a_g_kv, a_w_qb, a_w_kvb, a_w_o,
              b_w_qkv, b_lambda, b_g_sub, b_w_o,
              c_w_qkv, c_sink, c_w_o, g_final):
    Bn, S, _ = x.shape
    rows = S // GRID_W
    row = jnp.repeat(jnp.arange(rows, dtype=jnp.int32), GRID_W)
    col = jnp.tile(jnp.arange(GRID_W, dtype=jnp.int32), rows)
    silu_c = jax.nn.silu(c)
    silu_cc = jax.nn.silu(c_ctx)

    for i in range(DEPTH):
        need_ctx = i < DEPTH - 1
        mod_l = (silu_c @ w_mod[i] + b_mod[i]).reshape(Bn, N_MOD, 1, D_MODEL)
        mod_c = (silu_cc @ w_mod[i] + b_mod[i]).reshape(N_MOD, D_MODEL)

        x = ffn_half_step(x, g_norm[i, 0], mod_l[:, 0], mod_l[:, 1], mod_l[:, 2], w_ffn_in[i, 0], w_ffn_out[i, 0])
        ctx = ffn_half_step(ctx, g_norm[i, 0], mod_c[0], mod_c[1], mod_c[2], w_ffn_in[i, 0], w_ffn_out[i, 0])

        h_l = modulate(rmsnorm(x, g_norm[i, 1]), mod_l[:, 3], mod_l[:, 4])
        h_c = modulate(rmsnorm(ctx, g_norm[i, 1]), mod_c[3], mod_c[4])
        kind, j = i % N_MIXERS, i // N_MIXERS
        if kind == 0:
            y_l, y_c = mla_mixer(h_l, h_c, row, col, a_w_in[j], a_g_q[j], a_g_kv[j],
                                 a_w_qb[j], a_w_kvb[j], a_w_o[j], need_ctx)
        elif kind == 1:
            y_l, y_c = diff_mixer(h_l, h_c, row, col, b_w_qkv[j], b_lambda[j], b_g_sub[j], b_w_o[j], i, need_ctx)
        else:
            y_l, y_c = swa_mixer(h_l, h_c, row, col, c_w_qkv[j], c_sink[j], c_w_o[j], need_ctx)
        x = x + mod_l[:, 5] * y_l

        x = ffn_half_step(x, g_norm[i, 2], mod_l[:, 6], mod_l[:, 7], mod_l[:, 8], w_ffn_in[i, 1], w_ffn_out[i, 1])
        if need_ctx:
            ctx = ctx + mod_c[5] * y_c
            ctx = ffn_half_step(ctx, g_norm[i, 2], mod_c[6], mod_c[7], mod_c[8], w_ffn_in[i, 1], w_ffn_out[i, 1])

    return rmsnorm(x, g_final)
```

```python
import functools
import math

import jax
import jax.numpy as jnp
from jax import lax
from jax.experimental import pallas as pl
from jax.experimental.pallas import tpu as pltpu

F32 = jnp.float32
BF16 = jnp.bfloat16

N_MOD = 9
N_MIXERS = 3
GRID_W = 64
ROPE_THETA = 10000.0
NORM_EPS = 1e-6
A_NOPE = 128
A_ROPE = 64
A_V = 128
B_HEAD_DIM = 64
C_HEAD_DIM = 64
C_GROUP = 4
C_WINDOW = 128

LANE = 128
ROW_TILE = 256
VMEM_LIMIT_BYTES = 56 * 1024 * 1024


def _params(n_axes):
    return pltpu.CompilerParams(
        dimension_semantics=("parallel",) * n_axes,
        vmem_limit_bytes=VMEM_LIMIT_BYTES)


def _const_spec(shape):
    nd = len(shape)
    return pl.BlockSpec(shape, lambda *_: (0,) * nd, pipeline_mode=pl.Buffered(1))


def _rms(x, g):
    ms = jnp.mean(x * x, axis=-1, keepdims=True)
    return x * lax.rsqrt(ms + NORM_EPS) * g


def _dot(a, b):
    return jnp.dot(a, b, preferred_element_type=F32)


def _dot_nt(a, b):
    return lax.dot_general(a, b, (((1,), (1,)), ((), ())), preferred_element_type=F32)


def _rope(blk, cos, sin):
    lane = lax.broadcasted_iota(jnp.int32, blk.shape, 1)
    first_half = (lane % 32) < 16
    partner = jnp.where(first_half,
                        pltpu.roll(blk, LANE - 16, axis=1),
                        pltpu.roll(blk, 16, axis=1))
    return blk * cos + partner * sin


def _mod_kernel(c_ref, w_ref, b_ref, o_ref):
    c = c_ref[...]
    sc = c * jax.nn.sigmoid(c)
    o_ref[0] = jnp.dot(sc, w_ref[0], preferred_element_type=F32,
                       precision=lax.Precision.HIGHEST) + b_ref[0]


def _modulation(cvec, w_mod, b_mod):
    depth, d, nm = w_mod.shape
    rows = cvec.shape[0]
    col_tile = 1024
    return pl.pallas_call(
        _mod_kernel,
        grid=(depth, nm // col_tile),
        in_specs=[
            pl.BlockSpec((rows, d), lambda i, j: (0, 0)),
            pl.BlockSpec((1, d, col_tile), lambda i, j: (i, 0, j)),
            pl.BlockSpec((1, 1, col_tile), lambda i, j: (i, 0, j)),
        ],
        out_specs=pl.BlockSpec((1, rows, col_tile), lambda i, j: (i, 0, j)),
        out_shape=jax.ShapeDtypeStruct((depth, rows, nm), F32),
        compiler_params=_params(2),
        name="adaln_modulation",
    )(cvec, w_mod, b_mod.reshape(depth, 1, nm))


def _ffn_kernel(*refs, d_ff, mod_row, has_attn, final_norm):
    it = iter(refs)
    x_ref, mod_ref, g_ref, win_ref, wout_ref = (next(it) for _ in range(5))
    y_ref = wo_ref = gf_ref = None
    if has_attn:
        y_ref, wo_ref = next(it), next(it)
    if final_norm:
        gf_ref = next(it)
    o_ref = next(it)

    x = x_ref[0]
    mod = mod_ref[0, 0]
    if has_attn:
        x = x + mod[5:6] * _dot(y_ref[0], wo_ref[...])
    shift, scale, gate = (mod[mod_row + k:mod_row + k + 1] for k in range(3))
    h = (_rms(x, g_ref[...]) * (1.0 + scale) + shift).astype(BF16)
    gt = _dot(h, win_ref[:, :d_ff])
    up = _dot(h, win_ref[:, d_ff:])
    act = (gt * jax.nn.sigmoid(gt) * up).astype(BF16)
    out = x + 0.5 * gate * _dot(act, wout_ref[...])
    if final_norm:
        out = _rms(out, gf_ref[...])
    o_ref[0] = out


def _ffn(x_all, mod_i, g, w_in, w_out, *, mod_row, n_ctx_tiles, skip_ctx=False,
         attn=None, g_final=None):
    b, l, d = x_all.shape
    d_ff = w_out.shape[0]
    t0 = n_ctx_tiles if skip_ctx else 0
    n_tiles = l // ROW_TILE - t0

    def row_map(bi, t):
        return (bi, t + t0, 0)

    def mod_map(bi, t):
        return (bi, jnp.where(t + t0 >= n_ctx_tiles, 1, 0), 0, 0)

    in_specs = [
        pl.BlockSpec((1, ROW_TILE, d), row_map),
        pl.BlockSpec((1, 1, N_MOD, d), mod_map),
        _const_spec((1, d)),
        _const_spec((d, 2 * d_ff)),
        _const_spec((d_ff, d)),
    ]
    args = [x_all, mod_i, g.reshape(1, d), w_in, w_out]
    if attn is not None:
        y, w_o, y_has_ctx = attn
        y_t0 = t0 if y_has_ctx else 0
        in_specs += [pl.BlockSpec((1, ROW_TILE, y.shape[-1]), lambda bi, t: (bi, t + y_t0, 0)),
                     _const_spec(w_o.shape)]
        args += [y, w_o]
    if g_final is not None:
        in_specs.append(_const_spec((1, d)))
        args.append(g_final.reshape(1, d))
    return pl.pallas_call(
        functools.partial(_ffn_kernel, d_ff=d_ff, mod_row=mod_row,
                          has_attn=attn is not None, final_norm=g_final is not None),
        grid=(b, n_tiles),
        in_specs=in_specs,
        out_specs=pl.BlockSpec((1, ROW_TILE, d), lambda bi, t: (bi, t, 0)),
        out_shape=jax.ShapeDtypeStruct((b, n_tiles * ROW_TILE, d), F32),
        compiler_params=_params(2),
        name="ffn_half_step",
    )(*args)


def _mla_proj_kernel(x_ref, mod_ref, g_ref, win_ref, gq_ref, gkv_ref, wq_ref, wkn_ref, wv_ref,
                     cos_ref, sin_ref, q_ref, k_ref, v_ref, *, n_heads, q_lora, kv_lora, scale):
    mod = mod_ref[0, 0]
    h = (_rms(x_ref[0], g_ref[...]) * (1.0 + mod[4:5]) + mod[3:4]).astype(BF16)
    a = _dot(h, win_ref[...])
    cq = _rms(a[:, :q_lora], gq_ref[...]).astype(BF16)
    ckv = _rms(a[:, q_lora:q_lora + kv_lora], gkv_ref[...]).astype(BF16)
    cos, sin = cos_ref[...], sin_ref[...]
    k_rope = _rope(a[:, q_lora + kv_lora:], cos, sin).astype(BF16)
    q = _dot(cq, wq_ref[...])
    k_nope = _dot(ckv, wkn_ref[...])
    v_ref[0] = _dot(ckv, wv_ref[...]).astype(BF16)
    hd = 2 * LANE
    for hh in range(n_heads):
        q_ref[0, :, hh * hd:hh * hd + LANE] = (q[:, hh * hd:hh * hd + LANE] * scale).astype(BF16)
        q_ref[0, :, hh * hd + LANE:(hh + 1) * hd] = (
            _rope(q[:, hh * hd + LANE:(hh + 1) * hd], cos, sin) * scale).astype(BF16)
        k_ref[0, :, hh * hd:hh * hd + LANE] = k_nope[:, hh * LANE:(hh + 1) * LANE].astype(BF16)
        k_ref[0, :, hh * hd + LANE:(hh + 1) * hd] = k_rope


def _qkv_proj_kernel(x_ref, mod_ref, g_ref, w_ref, cos_ref, sin_ref, q_ref, k_ref, v_ref,
                     *, q_width, k_width, scale):
    mod = mod_ref[0, 0]
    h = (_rms(x_ref[0], g_ref[...]) * (1.0 + mod[4:5]) + mod[3:4]).astype(BF16)
    qkv = _dot(h, w_ref[...])
    cos, sin = cos_ref[...], sin_ref[...]
    for j in range(q_width // LANE):
        q_ref[0, :, j * LANE:(j + 1) * LANE] = (
            _rope(qkv[:, j * LANE:(j + 1) * LANE], cos, sin) * scale).astype(BF16)
    for j in range(k_width // LANE):
        c0 = q_width + j * LANE
        k_ref[0, :, j * LANE:(j + 1) * LANE] = _rope(qkv[:, c0:c0 + LANE], cos, sin).astype(BF16)
    v_ref[0] = qkv[:, q_width + k_width:].astype(BF16)


def _row_specs(b, l, d, n_ctx_tiles):
    def row_map(bi, t):
        return (bi, t, 0)

    def mod_map(bi, t):
        return (bi, jnp.where(t >= n_ctx_tiles, 1, 0), 0, 0)

    return [pl.BlockSpec((1, ROW_TILE, d), row_map),
            pl.BlockSpec((1, 1, N_MOD, d), mod_map),
            _const_spec((1, d))], row_map


def _rope_specs():
    return [pl.BlockSpec((ROW_TILE, LANE), lambda bi, t: (t, 0)),
            pl.BlockSpec((ROW_TILE, LANE), lambda bi, t: (t, 0))]


def _mla_proj(x_all, mod_i, g, w_in, g_q, g_kv, w_q, w_kn, w_v, cos, sin, *, n_ctx_tiles, scale):
    b, l, d = x_all.shape
    q_lora, kv_lora = g_q.shape[-1], g_kv.shape[-1]
    n_heads = w_kn.shape[1] // LANE
    head_specs, row_map = _row_specs(b, l, d, n_ctx_tiles)
    widths = (w_q.shape[1], n_heads * 2 * LANE, w_v.shape[1])
    return pl.pallas_call(
        functools.partial(_mla_proj_kernel, n_heads=n_heads, q_lora=q_lora, kv_lora=kv_lora,
                          scale=scale),
        grid=(b, l // ROW_TILE),
        in_specs=head_specs + [_const_spec(w_in.shape), _const_spec((1, q_lora)),
                               _const_spec((1, kv_lora)), _const_spec(w_q.shape),
                               _const_spec(w_kn.shape), _const_spec(w_v.shape)] + _rope_specs(),
        out_specs=[pl.BlockSpec((1, ROW_TILE, w), row_map) for w in widths],
        out_shape=[jax.ShapeDtypeStruct((b, l, w), BF16) for w in widths],
        compiler_params=_params(2),
        name="mla_projection",
    )(x_all, mod_i, g.reshape(1, d), w_in, g_q.reshape(1, -1), g_kv.reshape(1, -1),
      w_q, w_kn, w_v, cos, sin)


def _qkv_proj(x_all, mod_i, g, w_qkv, cos, sin, *, q_width, k_width, n_ctx_tiles, scale, name):
    b, l, d = x_all.shape
    head_specs, row_map = _row_specs(b, l, d, n_ctx_tiles)
    widths = (q_width, k_width, w_qkv.shape[1] - q_width - k_width)
    return pl.pallas_call(
        functools.partial(_qkv_proj_kernel, q_width=q_width, k_width=k_width, scale=scale),
        grid=(b, l // ROW_TILE),
        in_specs=head_specs + [_const_spec(w_qkv.shape)] + _rope_specs(),
        out_specs=[pl.BlockSpec((1, ROW_TILE, w), row_map) for w in widths],
        out_shape=[jax.ShapeDtypeStruct((b, l, w), BF16) for w in widths],
        compiler_params=_params(2),
        name=name,
    )(x_all, mod_i, g.reshape(1, d), w_qkv, cos, sin)


def _softmax_pv(s, v):
    m = jnp.max(s, axis=-1, keepdims=True)
    p = jnp.exp(s - m)
    denom = jnp.sum(p, axis=-1, keepdims=True)
    return _dot(p.astype(BF16), v) / denom


def _mla_attn_kernel(q_ref, k_ref, v_ref, o_ref, *, n_ctx, q_tile0, n_ctx_tiles):
    qi = pl.program_id(2) + q_tile0
    q = q_ref[0]

    def attend(n_keys):
        s = _dot_nt(q, k_ref[0, :n_keys, :])
        o_ref[0] = _softmax_pv(s, v_ref[0, :n_keys, :]).astype(o_ref.dtype)

    if q_tile0 < n_ctx_tiles:
        @pl.when(qi < n_ctx_tiles)
        def _():
            attend(n_ctx)

        @pl.when(qi >= n_ctx_tiles)
        def _():
            attend(k_ref.shape[1])
    else:
        attend(k_ref.shape[1])


def _diff_attn_kernel(lam_ref, gsub_ref, q_ref, k_ref, v_ref, o_ref, *, n_ctx, q_tile0,
                      n_ctx_tiles, lam_init):
    qi = pl.program_id(2) + q_tile0
    q = q_ref[0]
    lane = lax.broadcasted_iota(jnp.int32, q.shape, 1)
    zero = jnp.zeros_like(q)
    q1 = jnp.where(lane < B_HEAD_DIM, q, zero)
    q2 = jnp.where(lane >= B_HEAD_DIM, q, zero)
    lp = lam_ref[...]
    lam = (jnp.exp(jnp.sum(lp[0:1] * lp[1:2], axis=-1, keepdims=True))
           - jnp.exp(jnp.sum(lp[2:3] * lp[3:4], axis=-1, keepdims=True)) + lam_init)

    def attend(n_keys):
        k = k_ref[0, :n_keys, :]
        s1 = _dot_nt(q1, k)
        s2 = _dot_nt(q2, k)
        e1 = jnp.exp(s1 - jnp.max(s1, axis=-1, keepdims=True))
        e2 = jnp.exp(s2 - jnp.max(s2, axis=-1, keepdims=True))
        r1 = 1.0 / jnp.sum(e1, axis=-1, keepdims=True)
        r2 = lam / jnp.sum(e2, axis=-1, keepdims=True)
        a = (e1 * r1 - e2 * r2).astype(BF16)
        o = _dot(a, v_ref[0, :n_keys, :])
        o_ref[0] = (_rms(o, gsub_ref[...]) * (1.0 - lam_init)).astype(o_ref.dtype)

    if q_tile0 < n_ctx_tiles:
        @pl.when(qi < n_ctx_tiles)
        def _():
            attend(n_ctx)

        @pl.when(qi >= n_ctx_tiles)
        def _():
            attend(k_ref.shape[1])
    else:
        attend(k_ref.shape[1])


def _swa_attn_kernel(sink_ref, q_ref, k_ref, v_ref, o_ref, *, n_ctx, q_tile0, n_ctx_tiles):
    kvh = pl.program_id(1)
    qi = pl.program_id(2) + q_tile0
    tq = q_ref.shape[1]
    n_lat = k_ref.shape[1] - n_ctx
    win = tq + 2 * C_WINDOW
    d = C_HEAD_DIM

    q = q_ref[0]
    lane = lax.broadcasted_iota(jnp.int32, (tq, LANE), 1)
    low = lane < d
    zero = jnp.zeros((tq, LANE), q.dtype)
    q_rows = jnp.concatenate(
        [jnp.where(low if g % 2 == 0 else ~low, q[:, (g // 2) * LANE:(g // 2 + 1) * LANE], zero)
         for g in range(C_GROUP)], axis=0)
    row_head = lax.broadcasted_iota(jnp.int32, (C_GROUP * tq, 1), 0) // tq
    sink = jnp.zeros((C_GROUP * tq, 1), F32)
    for g in range(C_GROUP):
        sink = jnp.where(row_head == g, sink_ref[kvh * C_GROUP + g], sink)

    def finish(s, v):
        m = jnp.maximum(jnp.max(s, axis=-1, keepdims=True), sink)
        p = jnp.exp(s - m)
        denom = jnp.sum(p, axis=-1, keepdims=True) + jnp.exp(sink - m)
        o = _dot(p.astype(BF16), v) / denom
        for pair in range(C_GROUP // 2):
            o_ref[0, :, pair * LANE:(pair + 1) * LANE] = jnp.where(
                low, o[2 * pair * tq:(2 * pair + 1) * tq],
                o[(2 * pair + 1) * tq:(2 * pair + 2) * tq]).astype(o_ref.dtype)

    def attend_ctx():
        finish(_dot_nt(q_rows, k_ref[0, :n_ctx, :]), v_ref[0, :n_ctx, :])

    def attend_lat():
        q0 = (qi - n_ctx_tiles) * tq
        k0 = pl.multiple_of(jnp.clip(q0 - C_WINDOW, 0, n_lat - win), C_WINDOW)
        k = jnp.concatenate([k_ref[0, :n_ctx, :], k_ref[0, pl.ds(n_ctx + k0, win), :]], axis=0)
        v = jnp.concatenate([v_ref[0, :n_ctx, :], v_ref[0, pl.ds(n_ctx + k0, win), :]], axis=0)
        s = _dot_nt(q_rows, k)
        col = lax.broadcasted_iota(jnp.int32, s.shape, 1)
        row = lax.broadcasted_iota(jnp.int32, s.shape, 0) % tq
        dist = (k0 + col - n_ctx) - (q0 + row)
        valid = (col < n_ctx) | (jnp.abs(dist) <= C_WINDOW)
        finish(jnp.where(valid, s, -1e30), v)

    if q_tile0 < n_ctx_tiles:
        pl.when(qi < n_ctx_tiles)(attend_ctx)
        pl.when(qi >= n_ctx_tiles)(attend_lat)
    else:
        attend_lat()


def _attention(body, q, k, v, *, n_groups, q_width, kv_width, out_width, need_ctx, n_ctx,
               extra=(), extra_specs=(), name):
    b, l, _ = q.shape
    n_ctx_tiles = n_ctx // ROW_TILE
    q_tile0 = 0 if need_ctx else n_ctx_tiles
    n_q = l // ROW_TILE - q_tile0
    k_width, v_width = kv_width
    return pl.pallas_call(
        functools.partial(body, n_ctx=n_ctx, q_tile0=q_tile0, n_ctx_tiles=n_ctx_tiles),
        grid=(b, n_groups, n_q),
        in_specs=list(extra_specs) + [
            pl.BlockSpec((1, ROW_TILE, q_width), lambda bi, h, t: (bi, t + q_tile0, h)),
            pl.BlockSpec((1, l, k_width), lambda bi, h, t: (bi, 0, h)),
            pl.BlockSpec((1, l, v_width), lambda bi, h, t: (bi, 0, h)),
        ],
        out_specs=pl.BlockSpec((1, ROW_TILE, out_width), lambda bi, h, t: (bi, t, h)),
        out_shape=jax.ShapeDtypeStruct((b, n_q * ROW_TILE, n_groups * out_width), BF16),
        compiler_params=_params(3),
        name=name,
    )(*extra, q, k, v)


def _prep_mla(w_in, w_qb, w_kvb):
    d, a_in = w_in.shape
    n_heads = w_qb.shape[1] // (A_NOPE + A_ROPE)
    w_in_p = jnp.pad(w_in, ((0, 0), (0, (-a_in) % LANE))).astype(BF16)
    q3 = w_qb.reshape(w_qb.shape[0], n_heads, A_NOPE + A_ROPE)
    q3 = jnp.pad(q3, ((0, 0), (0, 0), (0, 2 * LANE - (A_NOPE + A_ROPE))))
    w_q = q3.reshape(w_qb.shape[0], n_heads * 2 * LANE).astype(BF16)
    kv3 = w_kvb.reshape(w_kvb.shape[0], n_heads, A_NOPE + A_V)
    w_kn = kv3[:, :, :A_NOPE].reshape(w_kvb.shape[0], n_heads * A_NOPE).astype(BF16)
    w_v = kv3[:, :, A_NOPE:].reshape(w_kvb.shape[0], n_heads * A_V).astype(BF16)
    return w_in_p, w_q, w_kn, w_v


def _prep_swa(w_qkv, n_q_heads):
    d = w_qkv.shape[0]
    nq = n_q_heads * C_HEAD_DIM
    n_kv = n_q_heads // C_GROUP
    nkv = n_kv * C_HEAD_DIM

    def dup(w):
        w3 = w.reshape(d, n_kv, 1, C_HEAD_DIM)
        return jnp.broadcast_to(w3, (d, n_kv, 2, C_HEAD_DIM)).reshape(d, n_kv * LANE)

    return jnp.concatenate(
        [w_qkv[:, :nq], dup(w_qkv[:, nq:nq + nkv]), dup(w_qkv[:, nq + nkv:])], axis=1).astype(BF16)


def _rope_tables(n_ctx, seq):
    quarter = 16
    t = jnp.arange(seq, dtype=jnp.int32)
    row, col = (t // GRID_W).astype(F32), (t % GRID_W).astype(F32)
    inv_freq = ROPE_THETA ** (-jnp.arange(quarter, dtype=F32) / quarter)
    ang = jnp.stack([row[:, None] * inv_freq, col[:, None] * inv_freq], axis=1)
    cos, sin = jnp.cos(ang), jnp.sin(ang)
    cos64 = jnp.stack([cos, cos], axis=2).reshape(seq, 64)
    sin64 = jnp.stack([-sin, sin], axis=2).reshape(seq, 64)
    cos_t = jnp.concatenate([jnp.ones((n_ctx, 64), F32), cos64], axis=0)
    sin_t = jnp.concatenate([jnp.zeros((n_ctx, 64), F32), sin64], axis=0)
    return jnp.tile(cos_t, (1, 2)), jnp.tile(sin_t, (1, 2))


def kernel(x, c, ctx, c_ctx, w_mod, b_mod, g_norm, w_ffn_in, w_ffn_out, a_w_in, a_g_q, a_g_kv, a_w_qb, a_w_kvb, a_w_o, b_w_qkv, b_lambda, b_g_sub, b_w_o, c_w_qkv, c_sink, c_w_o, g_final):
    bn, seq, d = x.shape
    n_ctx = ctx.shape[1]
    depth = w_mod.shape[0]
    assert n_ctx % ROW_TILE == 0 and seq % ROW_TILE == 0 and d % LANE == 0
    assert seq >= ROW_TILE + 2 * C_WINDOW
    n_ctx_tiles = n_ctx // ROW_TILE

    rows = -(-(bn + 1) // 8) * 8
    cvec = jnp.concatenate([c, c_ctx[None], jnp.zeros((rows - bn - 1, d), F32)], axis=0)
    mod_all = _modulation(cvec, w_mod, b_mod).reshape(depth, rows, N_MOD, d)

    def layer_mod(i):
        lat = mod_all[i, :bn]
        cx = jnp.broadcast_to(mod_all[i, bn][None], lat.shape)
        return jnp.stack([cx, lat], axis=1)

    cos, sin = _rope_tables(n_ctx, seq)
    x_all = jnp.concatenate([ctx, x], axis=1)
    w_in_b = w_ffn_in.astype(BF16)
    w_out_b = w_ffn_out.astype(BF16)

    out = None
    for i in range(depth):
        need_ctx = i < depth - 1
        last = i == depth - 1
        mod_i = layer_mod(i)
        kind, j = i % N_MIXERS, i // N_MIXERS
        x_all = _ffn(x_all, mod_i, g_norm[i, 0], w_in_b[i, 0], w_out_b[i, 0], mod_row=0,
                     n_ctx_tiles=n_ctx_tiles)
        if kind == 0:
            n_heads = a_w_o.shape[1] // A_V
            w_in_p, w_q, w_kn, w_v = _prep_mla(a_w_in[j], a_w_qb[j], a_w_kvb[j])
            q, k, v = _mla_proj(x_all, mod_i, g_norm[i, 1], w_in_p, a_g_q[j], a_g_kv[j], w_q, w_kn,
                                w_v, cos, sin, n_ctx_tiles=n_ctx_tiles,
                                scale=(A_NOPE + A_ROPE) ** -0.5)
            y = _attention(_mla_attn_kernel, q, k, v, n_groups=n_heads, q_width=2 * LANE,
                           kv_width=(2 * LANE, LANE), out_width=LANE, need_ctx=need_ctx,
                           n_ctx=n_ctx, name="mla_attention")
            w_o = a_w_o[j]
        elif kind == 1:
            n_heads = b_w_o.shape[1] // (2 * B_HEAD_DIM)
            width = n_heads * 2 * B_HEAD_DIM
            q, k, v = _qkv_proj(x_all, mod_i, g_norm[i, 1], b_w_qkv[j].astype(BF16), cos, sin,
                                q_width=width, k_width=width, n_ctx_tiles=n_ctx_tiles,
                                scale=B_HEAD_DIM ** -0.5, name="diff_projection")
            lam_init = 0.8 - 0.6 * math.exp(-0.3 * i)
            y = _attention(
                functools.partial(_diff_attn_kernel, lam_init=lam_init), q, k, v,
                n_groups=n_heads, q_width=LANE, kv_width=(LANE, LANE), out_width=LANE,
                need_ctx=need_ctx, n_ctx=n_ctx,
                extra=(b_lambda[j], b_g_sub[j].reshape(1, -1)),
                extra_specs=(pl.BlockSpec(b_lambda[j].shape, lambda bi, h, t: (0, 0)),
                             pl.BlockSpec((1, 2 * B_HEAD_DIM), lambda bi, h, t: (0, 0))),
                name="diff_attention")
            w_o = b_w_o[j]
        else:
            n_q_heads = c_w_o.shape[1] // C_HEAD_DIM
            n_kv = n_q_heads // C_GROUP
            q, k, v = _qkv_proj(x_all, mod_i, g_norm[i, 1], _prep_swa(c_w_qkv[j], n_q_heads), cos,
                                sin, q_width=n_q_heads * C_HEAD_DIM, k_width=n_kv * LANE,
                                n_ctx_tiles=n_ctx_tiles, scale=C_HEAD_DIM ** -0.5,
                                name="swa_projection")
            y = _attention(
                _swa_attn_kernel, q, k, v, n_groups=n_kv, q_width=C_GROUP * C_HEAD_DIM,
                kv_width=(LANE, LANE), out_width=C_GROUP * C_HEAD_DIM, need_ctx=need_ctx,
                n_ctx=n_ctx, extra=(c_sink[j],),
                extra_specs=(pl.BlockSpec(memory_space=pltpu.SMEM),),
                name="swa_attention")
            w_o = c_w_o[j]
        x_all = _ffn(x_all, mod_i, g_norm[i, 2], w_in_b[i, 1], w_out_b[i, 1], mod_row=6,
                     n_ctx_tiles=n_ctx_tiles, skip_ctx=last,
                     attn=(y, w_o.astype(BF16), need_ctx),
                     g_final=g_final if last else None)
        out = x_all
    return out
```

```python
import functools
import math

import jax
import jax.numpy as jnp
from jax import lax
from jax.experimental import pallas as pl
from jax.experimental.pallas import tpu as pltpu

F32 = jnp.float32
BF16 = jnp.bfloat16

N_MOD = 9
N_MIXERS = 3
GRID_W = 64
ROPE_THETA = 10000.0
NORM_EPS = 1e-6
A_NOPE = 128
A_ROPE = 64
A_V = 128
B_HEAD_DIM = 64
C_HEAD_DIM = 64
C_GROUP = 4
C_WINDOW = 128
MASKED = -1e30
LOG2E = math.log2(math.e)

LANE = 128
ROW_TILE = 256
Q_TILE = 2 * ROW_TILE
VMEM_LIMIT_BYTES = 56 * 1024 * 1024


def _params(n_axes):
    return pltpu.CompilerParams(
        dimension_semantics=("parallel",) * n_axes,
        vmem_limit_bytes=VMEM_LIMIT_BYTES)


def _const_spec(shape):
    nd = len(shape)
    return pl.BlockSpec(shape, lambda *_: (0,) * nd, pipeline_mode=pl.Buffered(1))


def _rms(x, g):
    ms = jnp.mean(x * x, axis=-1, keepdims=True)
    return x * lax.rsqrt(ms + NORM_EPS) * g


def _dot(a, b):
    return jnp.dot(a, b, preferred_element_type=F32)


def _dot_nt(a, b):
    return lax.dot_general(a, b, (((1,), (1,)), ((), ())), preferred_element_type=F32)


def _rope(blk, cos, sin):
    lane = lax.broadcasted_iota(jnp.int32, blk.shape, 1)
    first_half = (lane % 32) < 16
    partner = jnp.where(first_half,
                        pltpu.roll(blk, LANE - 16, axis=1),
                        pltpu.roll(blk, 16, axis=1))
    return blk * cos + partner * sin


def _mod_kernel(c_ref, w_ref, b_ref, o_ref):
    c = c_ref[...]
    sc = c * jax.nn.sigmoid(c)
    o_ref[0] = jnp.dot(sc, w_ref[0], preferred_element_type=F32,
                       precision=lax.Precision.HIGHEST) + b_ref[0]


def _modulation(cvec, w_mod, b_mod):
    depth, d, nm = w_mod.shape
    rows = cvec.shape[0]
    col_tile = 1024
    return pl.pallas_call(
        _mod_kernel,
        grid=(depth, nm // col_tile),
        in_specs=[
            pl.BlockSpec((rows, d), lambda i, j: (0, 0)),
            pl.BlockSpec((1, d, col_tile), lambda i, j: (i, 0, j)),
            pl.BlockSpec((1, 1, col_tile), lambda i, j: (i, 0, j)),
        ],
        out_specs=pl.BlockSpec((1, rows, col_tile), lambda i, j: (i, 0, j)),
        out_shape=jax.ShapeDtypeStruct((depth, rows, nm), F32),
        compiler_params=_params(2),
        name="adaln_modulation",
    )(cvec, w_mod, b_mod.reshape(depth, 1, nm))


def _mod_spec(d, n_lat_tiles):
    return pl.BlockSpec((1, 1, N_MOD, d),
                        lambda bi, t: (bi, jnp.where(t < n_lat_tiles, 1, 0), 0, 0))


def _ffn_kernel(*refs, d_ff, mod_row, has_attn, final_norm):
    it = iter(refs)
    x_ref, mod_ref, g_ref, win_ref, wout_ref = (next(it) for _ in range(5))
    y_ref = wo_ref = gf_ref = None
    if has_attn:
        y_ref, wo_ref = next(it), next(it)
    if final_norm:
        gf_ref = next(it)
    o_ref = next(it)

    x = x_ref[0]
    mod = mod_ref[0, 0]
    if has_attn:
        x = x + mod[5:6] * _dot(y_ref[0], wo_ref[...])
    shift, scale, gate = (mod[mod_row + k:mod_row + k + 1] for k in range(3))
    h = (_rms(x, g_ref[...]) * (1.0 + scale) + shift).astype(BF16)
    gt = _dot(h, win_ref[:, :d_ff])
    up = _dot(h, win_ref[:, d_ff:])
    act = (gt * jax.nn.sigmoid(gt) * up).astype(BF16)
    out = x + 0.5 * gate * _dot(act, wout_ref[...])
    if final_norm:
        out = _rms(out, gf_ref[...])
    o_ref[0] = out


def _ffn(x_all, mod_i, g, w_in, w_out, *, mod_row, n_lat_tiles, latent_only=False,
         attn=None, g_final=None):
    b, l, d = x_all.shape
    d_ff = w_out.shape[0]
    n_tiles = n_lat_tiles if latent_only else l // ROW_TILE

    def row_map(bi, t):
        return (bi, t, 0)

    in_specs = [
        pl.BlockSpec((1, ROW_TILE, d), row_map),
        _mod_spec(d, n_lat_tiles),
        _const_spec((1, d)),
        _const_spec((d, 2 * d_ff)),
        _const_spec((d_ff, d)),
    ]
    args = [x_all, mod_i, g.reshape(1, d), w_in, w_out]
    if attn is not None:
        y, w_o = attn
        in_specs += [pl.BlockSpec((1, ROW_TILE, y.shape[-1]), row_map), _const_spec(w_o.shape)]
        args += [y, w_o]
    if g_final is not None:
        in_specs.append(_const_spec((1, d)))
        args.append(g_final.reshape(1, d))
    return pl.pallas_call(
        functools.partial(_ffn_kernel, d_ff=d_ff, mod_row=mod_row,
                          has_attn=attn is not None, final_norm=g_final is not None),
        grid=(b, n_tiles),
        in_specs=in_specs,
        out_specs=pl.BlockSpec((1, ROW_TILE, d), row_map),
        out_shape=jax.ShapeDtypeStruct((b, n_tiles * ROW_TILE, d), F32),
        compiler_params=_params(2),
        name="ffn_half_step",
    )(*args)


def _mla_proj_kernel(x_ref, mod_ref, g_ref, win_ref, gq_ref, gkv_ref, wq_ref, wkn_ref, wv_ref,
                     cos_ref, sin_ref, q_ref, k_ref, v_ref, *, n_heads, q_lora, kv_lora, scale):
    mod = mod_ref[0, 0]
    h = (_rms(x_ref[0], g_ref[...]) * (1.0 + mod[4:5]) + mod[3:4]).astype(BF16)
    a = _dot(h, win_ref[...])
    cq = _rms(a[:, :q_lora], gq_ref[...]).astype(BF16)
    ckv = _rms(a[:, q_lora:q_lora + kv_lora], gkv_ref[...]).astype(BF16)
    cos, sin = cos_ref[...], sin_ref[...]
    k_rope = _rope(a[:, q_lora + kv_lora:], cos, sin).astype(BF16)
    q = _dot(cq, wq_ref[...])
    k_nope = _dot(ckv, wkn_ref[...])
    v_ref[0] = _dot(ckv, wv_ref[...]).astype(BF16)
    hd = 2 * LANE
    for hh in range(n_heads):
        q_ref[0, :, hh * hd:hh * hd + LANE] = (q[:, hh * hd:hh * hd + LANE] * scale).astype(BF16)
        q_ref[0, :, hh * hd + LANE:(hh + 1) * hd] = (
            _rope(q[:, hh * hd + LANE:(hh + 1) * hd], cos, sin) * scale).astype(BF16)
        k_ref[0, :, hh * hd:hh * hd + LANE] = k_nope[:, hh * LANE:(hh + 1) * LANE].astype(BF16)
        k_ref[0, :, hh * hd + LANE:(hh + 1) * hd] = k_rope


def _qkv_proj_kernel(x_ref, mod_ref, g_ref, w_ref, cos_ref, sin_ref, q_ref, k_ref, v_ref,
                     *, q_width, k_width, scale):
    mod = mod_ref[0, 0]
    h = (_rms(x_ref[0], g_ref[...]) * (1.0 + mod[4:5]) + mod[3:4]).astype(BF16)
    qkv = _dot(h, w_ref[...])
    cos, sin = cos_ref[...], sin_ref[...]
    for j in range(q_width // LANE):
        q_ref[0, :, j * LANE:(j + 1) * LANE] = (
            _rope(qkv[:, j * LANE:(j + 1) * LANE], cos, sin) * scale).astype(BF16)
    for j in range(k_width // LANE):
        c0 = q_width + j * LANE
        k_ref[0, :, j * LANE:(j + 1) * LANE] = _rope(qkv[:, c0:c0 + LANE], cos, sin).astype(BF16)
    v_ref[0] = qkv[:, q_width + k_width:].astype(BF16)


def _row_map(bi, t):
    return (bi, t, 0)


def _row_specs(d, n_lat_tiles):
    return [pl.BlockSpec((1, ROW_TILE, d), _row_map), _mod_spec(d, n_lat_tiles),
            _const_spec((1, d))]


def _rope_specs():
    return [pl.BlockSpec((ROW_TILE, LANE), lambda bi, t: (t, 0)),
            pl.BlockSpec((ROW_TILE, LANE), lambda bi, t: (t, 0))]


def _mla_proj(x_all, mod_i, g, w_in, g_q, g_kv, w_q, w_kn, w_v, cos, sin, *, n_lat_tiles, scale):
    b, l, d = x_all.shape
    q_lora, kv_lora = g_q.shape[-1], g_kv.shape[-1]
    n_heads = w_kn.shape[1] // LANE
    widths = (w_q.shape[1], n_heads * 2 * LANE, w_v.shape[1])
    return pl.pallas_call(
        functools.partial(_mla_proj_kernel, n_heads=n_heads, q_lora=q_lora, kv_lora=kv_lora,
                          scale=scale),
        grid=(b, l // ROW_TILE),
        in_specs=_row_specs(d, n_lat_tiles) + [
            _const_spec(w_in.shape), _const_spec((1, q_lora)), _const_spec((1, kv_lora)),
            _const_spec(w_q.shape), _const_spec(w_kn.shape), _const_spec(w_v.shape)] + _rope_specs(),
        out_specs=[pl.BlockSpec((1, ROW_TILE, w), _row_map) for w in widths],
        out_shape=[jax.ShapeDtypeStruct((b, l, w), BF16) for w in widths],
        compiler_params=_params(2),
        name="mla_projection",
    )(x_all, mod_i, g.reshape(1, d), w_in, g_q.reshape(1, -1), g_kv.reshape(1, -1),
      w_q, w_kn, w_v, cos, sin)


def _qkv_proj(x_all, mod_i, g, w_qkv, cos, sin, *, q_width, k_width, n_lat_tiles, scale, name):
    b, l, d = x_all.shape
    widths = (q_width, k_width, w_qkv.shape[1] - q_width - k_width)
    return pl.pallas_call(
        functools.partial(_qkv_proj_kernel, q_width=q_width, k_width=k_width, scale=scale),
        grid=(b, l // ROW_TILE),
        in_specs=_row_specs(d, n_lat_tiles) + [_const_spec(w_qkv.shape)] + _rope_specs(),
        out_specs=[pl.BlockSpec((1, ROW_TILE, w), _row_map) for w in widths],
        out_shape=[jax.ShapeDtypeStruct((b, l, w), BF16) for w in widths],
        compiler_params=_params(2),
        name=name,
    )(x_all, mod_i, g.reshape(1, d), w_qkv, cos, sin)


def _softmax_pv(s, v):
    m = jnp.max(s, axis=-1, keepdims=True)
    p = jnp.exp2(s - m)
    denom = jnp.sum(p, axis=-1, keepdims=True)
    return _dot(p.astype(BF16), v) / denom


def _row_halves(n_rows):
    step = min(n_rows, ROW_TILE)
    return [(r, r + step) for r in range(0, n_rows, step)]


def _mla_attn_kernel(q_ref, k_ref, v_ref, o_ref, *, ctx_queries):
    del ctx_queries
    k, v = k_ref[0], v_ref[0]
    for r0, r1 in _row_halves(q_ref.shape[1]):
        s = _dot_nt(q_ref[0, r0:r1, :], k)
        o_ref[0, r0:r1, :] = _softmax_pv(s, v).astype(o_ref.dtype)


def _diff_attn_kernel(lam_ref, gsub_ref, q_ref, k_ref, v_ref, o_ref, *, ctx_queries, lam_init):
    del ctx_queries
    k, v = k_ref[0], v_ref[0]
    lp = lam_ref[...]
    lam = (jnp.exp(jnp.sum(lp[0:1] * lp[1:2], axis=-1, keepdims=True))
           - jnp.exp(jnp.sum(lp[2:3] * lp[3:4], axis=-1, keepdims=True)) + lam_init)
    for r0, r1 in _row_halves(q_ref.shape[1]):
        q = q_ref[0, r0:r1, :]
        lane = lax.broadcasted_iota(jnp.int32, q.shape, 1)
        zero = jnp.zeros_like(q)
        o1 = _softmax_pv(_dot_nt(jnp.where(lane < B_HEAD_DIM, q, zero), k), v)
        o2 = _softmax_pv(_dot_nt(jnp.where(lane >= B_HEAD_DIM, q, zero), k), v)
        o = o1 - lam * o2
        o_ref[0, r0:r1, :] = (_rms(o, gsub_ref[...]) * (1.0 - lam_init)).astype(o_ref.dtype)


def _swa_attn_kernel(sink_ref, q_ref, k_ref, v_ref, o_ref, *, ctx_queries, n_lat):
    kvh = pl.program_id(1)
    tq = q_ref.shape[1]
    d = C_HEAD_DIM

    q = q_ref[0]
    lane = lax.broadcasted_iota(jnp.int32, (tq, LANE), 1)
    low = lane < d
    zero = jnp.zeros((tq, LANE), q.dtype)
    q_rows = jnp.concatenate(
        [jnp.where(low if g % 2 == 0 else ~low, q[:, (g // 2) * LANE:(g // 2 + 1) * LANE], zero)
         for g in range(C_GROUP)], axis=0)

    if ctx_queries:
        k, v = k_ref[0], v_ref[0]
        bias = None
    else:
        n_ctx = k_ref.shape[1] - n_lat
        win = tq + 2 * C_WINDOW
        q0 = pl.program_id(2) * tq
        k0 = pl.multiple_of(jnp.clip(q0 - C_WINDOW, 0, n_lat - win), C_WINDOW)
        k = jnp.concatenate([k_ref[0, pl.ds(k0, win), :], k_ref[0, n_lat:, :]], axis=0)
        v = jnp.concatenate([v_ref[0, pl.ds(k0, win), :], v_ref[0, n_lat:, :]], axis=0)
        col = lax.broadcasted_iota(jnp.int32, (tq, win + n_ctx), 1)
        row = lax.broadcasted_iota(jnp.int32, (tq, win + n_ctx), 0)
        valid = (col >= win) | (jnp.abs((k0 + col) - (q0 + row)) <= C_WINDOW)
        bias = jnp.where(valid, 0.0, MASKED)

    s = _dot_nt(q_rows, k)
    outs = []
    for g in range(C_GROUP):
        sg = s[g * tq:(g + 1) * tq]
        if bias is not None:
            sg = sg + bias
        sink = jnp.full((tq, 1), sink_ref[kvh * C_GROUP + g], F32) * LOG2E
        m = jnp.maximum(jnp.max(sg, axis=-1, keepdims=True), sink)
        p = jnp.exp2(sg - m)
        denom = jnp.sum(p, axis=-1, keepdims=True) + jnp.exp2(sink - m)
        outs.append(_dot(p.astype(BF16), v) / denom)
    for pair in range(C_GROUP // 2):
        o_ref[0, :, pair * LANE:(pair + 1) * LANE] = jnp.where(
            low, outs[2 * pair], outs[2 * pair + 1]).astype(o_ref.dtype)


def _attention(body, q, k, v, *, n_groups, q_width, kv_width, out_width, need_ctx, n_lat,
               extra=(), extra_specs=(), name):
    b, l, _ = q.shape
    n_ctx = l - n_lat
    k_width, v_width = kv_width
    out_rows = l if need_ctx else n_lat
    out_shape = jax.ShapeDtypeStruct((b, out_rows, n_groups * out_width), BF16)

    def call(q_tile, n_q, q_block0, kv_rows, kv_block, aliased):
        in_specs = list(extra_specs) + [
            pl.BlockSpec((1, q_tile, q_width), lambda bi, h, t: (bi, t + q_block0, h)),
            pl.BlockSpec((1, kv_rows, k_width), lambda bi, h, t: (bi, kv_block, h)),
            pl.BlockSpec((1, kv_rows, v_width), lambda bi, h, t: (bi, kv_block, h)),
        ]
        args = list(extra) + [q, k, v]
        aliases = {}
        if aliased is not None:
            in_specs.append(pl.BlockSpec(memory_space=pl.ANY))
            aliases = {len(args): 0}
            args.append(aliased)
        kern = functools.partial(body, ctx_queries=aliased is not None)
        if aliased is not None:
            kern = _drop_last_input(kern, len(args))
        return pl.pallas_call(
            kern,
            grid=(b, n_groups, n_q),
            in_specs=in_specs,
            out_specs=pl.BlockSpec((1, q_tile, out_width), lambda bi, h, t: (bi, t + q_block0, h)),
            out_shape=out_shape,
            input_output_aliases=aliases,
            compiler_params=_params(3),
            name=name + ("_ctx" if aliased is not None else ""),
        )(*args)

    y = call(Q_TILE, n_lat // Q_TILE, 0, l, 0, None)
    if need_ctx:
        y = call(ROW_TILE, n_ctx // ROW_TILE, n_lat // ROW_TILE, n_ctx, n_lat // n_ctx, y)
    return y


def _drop_last_input(kern, n_inputs):
    def wrapped(*refs):
        return kern(*refs[:n_inputs - 1], *refs[n_inputs:])
    return wrapped


def _prep_mla(w_in, w_qb, w_kvb):
    d, a_in = w_in.shape
    n_heads = w_qb.shape[1] // (A_NOPE + A_ROPE)
    w_in_p = jnp.pad(w_in, ((0, 0), (0, (-a_in) % LANE))).astype(BF16)
    q3 = w_qb.reshape(w_qb.shape[0], n_heads, A_NOPE + A_ROPE)
    q3 = jnp.pad(q3, ((0, 0), (0, 0), (0, 2 * LANE - (A_NOPE + A_ROPE))))
    w_q = q3.reshape(w_qb.shape[0], n_heads * 2 * LANE).astype(BF16)
    kv3 = w_kvb.reshape(w_kvb.shape[0], n_heads, A_NOPE + A_V)
    w_kn = kv3[:, :, :A_NOPE].reshape(w_kvb.shape[0], n_heads * A_NOPE).astype(BF16)
    w_v = kv3[:, :, A_NOPE:].reshape(w_kvb.shape[0], n_heads * A_V).astype(BF16)
    return w_in_p, w_q, w_kn, w_v


def _prep_swa(w_qkv, n_q_heads):
    d = w_qkv.shape[0]
    nq = n_q_heads * C_HEAD_DIM
    n_kv = n_q_heads // C_GROUP
    nkv = n_kv * C_HEAD_DIM

    def dup(w):
        w3 = w.reshape(d, n_kv, 1, C_HEAD_DIM)
        return jnp.broadcast_to(w3, (d, n_kv, 2, C_HEAD_DIM)).reshape(d, n_kv * LANE)

    return jnp.concatenate(
        [w_qkv[:, :nq], dup(w_qkv[:, nq:nq + nkv]), dup(w_qkv[:, nq + nkv:])], axis=1).astype(BF16)


def _rope_tables(seq, n_ctx):
    quarter = 16
    t = jnp.arange(seq, dtype=jnp.int32)
    row, col = (t // GRID_W).astype(F32), (t % GRID_W).astype(F32)
    inv_freq = ROPE_THETA ** (-jnp.arange(quarter, dtype=F32) / quarter)
    ang = jnp.stack([row[:, None] * inv_freq, col[:, None] * inv_freq], axis=1)
    cos, sin = jnp.cos(ang), jnp.sin(ang)
    cos64 = jnp.stack([cos, cos], axis=2).reshape(seq, 64)
    sin64 = jnp.stack([-sin, sin], axis=2).reshape(seq, 64)
    cos_t = jnp.concatenate([cos64, jnp.ones((n_ctx, 64), F32)], axis=0)
    sin_t = jnp.concatenate([sin64, jnp.zeros((n_ctx, 64), F32)], axis=0)
    return jnp.tile(cos_t, (1, 2)), jnp.tile(sin_t, (1, 2))


def kernel(x, c, ctx, c_ctx, w_mod, b_mod, g_norm, w_ffn_in, w_ffn_out, a_w_in, a_g_q, a_g_kv, a_w_qb, a_w_kvb, a_w_o, b_w_qkv, b_lambda, b_g_sub, b_w_o, c_w_qkv, c_sink, c_w_o, g_final):
    bn, seq, d = x.shape
    n_ctx = ctx.shape[1]
    depth = w_mod.shape[0]
    assert n_ctx % ROW_TILE == 0 and seq % Q_TILE == 0 and seq % n_ctx == 0 and d % LANE == 0
    assert seq >= Q_TILE + 2 * C_WINDOW
    n_lat_tiles = seq // ROW_TILE

    rows = -(-(bn + 1) // 8) * 8
    cvec = jnp.concatenate([c, c_ctx[None], jnp.zeros((rows - bn - 1, d), F32)], axis=0)
    mod_all = _modulation(cvec, w_mod, b_mod).reshape(depth, rows, N_MOD, d)

    def layer_mod(i):
        lat = mod_all[i, :bn]
        cx = jnp.broadcast_to(mod_all[i, bn][None], lat.shape)
        return jnp.stack([cx, lat], axis=1)

    cos, sin = _rope_tables(seq, n_ctx)
    x_all = jnp.concatenate([x, ctx], axis=1)
    w_in_b = w_ffn_in.astype(BF16)
    w_out_b = w_ffn_out.astype(BF16)

    for i in range(depth):
        need_ctx = i < depth - 1
        last = i == depth - 1
        mod_i = layer_mod(i)
        kind, j = i % N_MIXERS, i // N_MIXERS
        x_all = _ffn(x_all, mod_i, g_norm[i, 0], w_in_b[i, 0], w_out_b[i, 0], mod_row=0,
                     n_lat_tiles=n_lat_tiles)
        if kind == 0:
            n_heads = a_w_o.shape[1] // A_V
            w_in_p, w_q, w_kn, w_v = _prep_mla(a_w_in[j], a_w_qb[j], a_w_kvb[j])
            q, k, v = _mla_proj(x_all, mod_i, g_norm[i, 1], w_in_p, a_g_q[j], a_g_kv[j], w_q, w_kn,
                                w_v, cos, sin, n_lat_tiles=n_lat_tiles,
                                scale=(A_NOPE + A_ROPE) ** -0.5 * LOG2E)
            y = _attention(_mla_attn_kernel, q, k, v, n_groups=n_heads, q_width=2 * LANE,
                           kv_width=(2 * LANE, LANE), out_width=LANE, need_ctx=need_ctx,
                           n_lat=seq, name="mla_attention")
            w_o = a_w_o[j]
        elif kind == 1:
            n_heads = b_w_o.shape[1] // (2 * B_HEAD_DIM)
            width = n_heads * 2 * B_HEAD_DIM
            q, k, v = _qkv_proj(x_all, mod_i, g_norm[i, 1], b_w_qkv[j].astype(BF16), cos, sin,
                                q_width=width, k_width=width, n_lat_tiles=n_lat_tiles,
                                scale=B_HEAD_DIM ** -0.5 * LOG2E, name="diff_projection")
            lam_init = 0.8 - 0.6 * math.exp(-0.3 * i)
            y = _attention(
                functools.partial(_diff_attn_kernel, lam_init=lam_init), q, k, v,
                n_groups=n_heads, q_width=LANE, kv_width=(LANE, LANE), out_width=LANE,
                need_ctx=need_ctx, n_lat=seq,
                extra=(b_lambda[j], b_g_sub[j].reshape(1, -1)),
                extra_specs=(pl.BlockSpec(b_lambda[j].shape, lambda bi, h, t: (0, 0)),
                             pl.BlockSpec((1, 2 * B_HEAD_DIM), lambda bi, h, t: (0, 0))),
                name="diff_attention")
            w_o = b_w_o[j]
        else:
            n_q_heads = c_w_o.shape[1] // C_HEAD_DIM
            n_kv = n_q_heads // C_GROUP
            q, k, v = _qkv_proj(x_all, mod_i, g_norm[i, 1], _prep_swa(c_w_qkv[j], n_q_heads), cos,
                                sin, q_width=n_q_heads * C_HEAD_DIM, k_width=n_kv * LANE,
                                n_lat_tiles=n_lat_tiles, scale=C_HEAD_DIM ** -0.5 * LOG2E,
                                name="swa_projection")
            y = _attention(
                functools.partial(_swa_attn_kernel, n_lat=seq), q, k, v, n_groups=n_kv,
                q_width=C_GROUP * C_HEAD_DIM, kv_width=(LANE, LANE),
                out_width=C_GROUP * C_HEAD_DIM, need_ctx=need_ctx, n_lat=seq,
                extra=(c_sink[j],), extra_specs=(pl.BlockSpec(memory_space=pltpu.SMEM),),
                name="swa_attention")
            w_o = c_w_o[j]
        x_all = _ffn(x_all, mod_i, g_norm[i, 2], w_in_b[i, 1], w_out_b[i, 1], mod_row=6,
                     n_lat_tiles=n_lat_tiles, latent_only=last, attn=(y, w_o.astype(BF16)),
                     g_final=g_final if last else None)
    return x_all
```
